```python
import math
import jax, jax.numpy as jnp
from jax import lax
import numpy as np

D_MODEL = 1024
BATCH = 4
SEQ = 4096
DEPTH = 2

HEAD_DIM = 64
SB_HEADS = 8
SW_HEADS = 8
SW_KV_HEADS = 2
SW_GROUP = SW_HEADS // SW_KV_HEADS
WINDOW = 128
BLOCK = 128
D_FF = 4 * D_MODEL
PLE_DIM = 256
N_BUCKETS = 32
MAX_DISTANCE = 128
EPS = 1e-6

SB_W = SB_HEADS * HEAD_DIM
SW_QW = SW_HEADS * HEAD_DIM
SW_KVW = SW_KV_HEADS * HEAD_DIM
IN_COLS = 3 * SB_W + SW_QW + 2 * SW_KVW + 2 * D_MODEL

kernel_name = "stick_breaking_swa_sink_hybrid_block"


def rmsnorm(x, g):
    xf = x.astype(jnp.float32)
    r = lax.rsqrt(jnp.mean(xf * xf, axis=-1, keepdims=True) + EPS)
    return (xf * r).astype(x.dtype) * g


def t5_causal_bucket(dist):
    max_exact = N_BUCKETS // 2
    d = jnp.maximum(dist, 0)
    df = jnp.maximum(d, 1).astype(jnp.float32)
    large = max_exact + (jnp.log(df / max_exact) / math.log(MAX_DISTANCE / max_exact)
                         * (N_BUCKETS - max_exact)).astype(jnp.int32)
    large = jnp.minimum(large, N_BUCKETS - 1)
    return jnp.where(d < max_exact, d, large)


def stick_breaking_attention(q, k, v):
    b_, s_, h_, dh = q.shape
    nb = s_ // BLOCK
    scale = dh ** -0.5
    qb = q.reshape(b_, nb, BLOCK, h_, dh).transpose(1, 0, 3, 2, 4)
    kpos = jnp.arange(s_)

    def one_block(args):
        qblk, n = args
        z = jnp.einsum('bhqd,bshd->bhqs', qblk, k).astype(jnp.float32) * scale
        qpos = n * BLOCK + jnp.arange(BLOCK)
        causal = kpos[None, :] < qpos[:, None]
        log_beta = jax.nn.log_sigmoid(z)
        log_1mb = jnp.where(causal, jax.nn.log_sigmoid(-z), 0.0)
        suffix = lax.cumsum(log_1mb, axis=3, reverse=True) - log_1mb
        a = jnp.where(causal, jnp.exp(log_beta + suffix), 0.0)
        return jnp.einsum('bhqs,bshd->bqhd', a.astype(v.dtype), v)

    o = lax.map(one_block, (qb, jnp.arange(nb)))
    return o.transpose(1, 0, 2, 3, 4).reshape(b_, s_, h_ * dh)


def sliding_window_sink_attention(q, k, v, sinks, bias):
    b_, s_ = q.shape[:2]
    nb = s_ // BLOCK
    scale = HEAD_DIM ** -0.5
    qb = q.reshape(b_, nb, BLOCK, SW_KV_HEADS, SW_GROUP, HEAD_DIM)
    pad = jnp.zeros((b_, BLOCK, SW_KV_HEADS, HEAD_DIM), k.dtype)
    kp = jnp.concatenate([pad, k], axis=1).reshape(b_, nb + 1, BLOCK, SW_KV_HEADS, HEAD_DIM)
    vp = jnp.concatenate([pad, v], axis=1).reshape(b_, nb + 1, BLOCK, SW_KV_HEADS, HEAD_DIM)
    kb = jnp.concatenate([kp[:, :-1], kp[:, 1:]], axis=2)
    vb = jnp.concatenate([vp[:, :-1], vp[:, 1:]], axis=2)
    s = jnp.einsum('bnqkgd,bnskd->bnkgqs', qb, kb).astype(jnp.float32) * scale + bias
    i = jnp.arange(BLOCK)[:, None]
    j = jnp.arange(2 * BLOCK)[None, :]
    dist = BLOCK + i - j
    band = (dist >= 0) & (dist < WINDOW)
    key_abs = (jnp.arange(nb)[:, None, None] - 1) * BLOCK + j[None]
    valid = band[None] & (key_abs >= 0)
    s = jnp.where(valid[None, :, None, None], s, -jnp.inf)
    sink = sinks.astype(jnp.float32).reshape(SW_KV_HEADS, SW_GROUP)[None, None, :, :, None, None]
    m = jnp.maximum(jnp.max(s, axis=-1, keepdims=True), sink)
    e = jnp.exp(s - m)
    denom = jnp.sum(e, axis=-1, keepdims=True) + jnp.exp(sink - m)
    pr = (e / denom).astype(v.dtype)
    o = jnp.einsum('bnkgqs,bnskd->bnqkgd', pr, vb)
    return o.reshape(b_, s_, SW_HEADS * HEAD_DIM)


def window_bias(rel_bias):
    i = jnp.arange(BLOCK)[:, None]
    j = jnp.arange(2 * BLOCK)[None, :]
    dist = BLOCK + i - j
    bias = rel_bias[t5_causal_bucket(dist)]
    return bias.transpose(2, 0, 1).reshape(SW_KV_HEADS, SW_GROUP, BLOCK, 2 * BLOCK)


def setup_inputs(seed: int = 0) -> dict:
    key = jax.random.key(seed)
    ks = jax.random.split(key, 20)
    f32 = jnp.float32

    def nrm(k, shape, fan_in):
        return jax.random.normal(k, shape, f32) * (fan_in ** -0.5)

    return {
        "x": jax.random.normal(ks[0], (BATCH, SEQ, D_MODEL), f32),
        "p": jax.random.normal(ks[1], (DEPTH, BATCH, SEQ, PLE_DIM), f32),
        "w_in": nrm(ks[2], (DEPTH, D_MODEL, IN_COLS), D_MODEL),
        "w_up_a": nrm(ks[3], (DEPTH, SB_W, D_MODEL), SB_W),
        "w_up_b": nrm(ks[4], (DEPTH, SW_QW, D_MODEL), SW_QW),
        "w_o": nrm(ks[5], (DEPTH, D_MODEL, D_MODEL), D_MODEL),
        "w_ff1": nrm(ks[6], (DEPTH, D_MODEL, D_FF), D_MODEL),
        "w_ff2": nrm(ks[7], (DEPTH, D_FF, D_MODEL), D_FF),
        "w_pe": nrm(ks[8], (DEPTH, PLE_DIM, D_MODEL), PLE_DIM),
        "w_pg": nrm(ks[9], (DEPTH, D_MODEL, D_MODEL), D_MODEL),
        "g_mix": 1.0 + 0.01 * jax.random.normal(ks[10], (DEPTH, D_MODEL), f32),
        "g_mlp": 1.0 + 0.01 * jax.random.normal(ks[11], (DEPTH, D_MODEL), f32),
        "g_pe": 1.0 + 0.01 * jax.random.normal(ks[12], (DEPTH, D_MODEL), f32),
        "g_final": 1.0 + 0.01 * jax.random.normal(ks[13], (D_MODEL,), f32),
        "sinks": 0.5 * jax.random.normal(ks[14], (DEPTH, SW_HEADS), f32),
        "rel_bias": 0.5 * jax.random.normal(ks[15], (N_BUCKETS, SW_HEADS), f32),
    }


def reference(x, p, w_in, w_up_a, w_up_b, w_o, w_ff1, w_ff2, w_pe, w_pg,
              g_mix, g_mlp, g_pe, g_final, sinks, rel_bias):
    b_, s_, _ = x.shape
    bias = window_bias(rel_bias)
    o1 = SB_W
    o2 = o1 + SB_W
    o3 = o2 + SB_W
    o4 = o3 + SW_QW
    o5 = o4 + SW_KVW
    o6 = o5 + SW_KVW
    o7 = o6 + D_MODEL
    for i in range(DEPTH):
        h = rmsnorm(x, g_mix[i])
        proj = h @ w_in[i]
        q_a = proj[..., :o1].reshape(b_, s_, SB_HEADS, HEAD_DIM)
        k_a = proj[..., o1:o2].reshape(b_, s_, SB_HEADS, HEAD_DIM)
        v_a = proj[..., o2:o3].reshape(b_, s_, SB_HEADS, HEAD_DIM)
        q_b = proj[..., o3:o4].reshape(b_, s_, SW_HEADS, HEAD_DIM)
        k_b = proj[..., o4:o5].reshape(b_, s_, SW_KV_HEADS, HEAD_DIM)
        v_b = proj[..., o5:o6].reshape(b_, s_, SW_KV_HEADS, HEAD_DIM)
        gate_a = proj[..., o6:o7]
        gate_b = proj[..., o7:]
        y_a = stick_breaking_attention(q_a, k_a, v_a) @ w_up_a[i]
        y_b = sliding_window_sink_attention(q_b, k_b, v_b, sinks[i], bias) @ w_up_b[i]
        merged = jax.nn.sigmoid(gate_a) * y_a + jax.nn.sigmoid(gate_b) * y_b
        x = x + merged @ w_o[i]
        h = rmsnorm(x, g_mlp[i])
        x = x + jnp.square(jax.nn.relu(h @ w_ff1[i])) @ w_ff2[i]
        pe = p[i] @ w_pe[i]
        x = x + pe * jax.nn.sigmoid(rmsnorm(x, g_pe[i]) @ w_pg[i])
    return rmsnorm(x, g_final)
```

```python
import functools
import math

import numpy as np
import jax
import jax.numpy as jnp
from jax import lax
from jax.experimental import pallas as pl
from jax.experimental.pallas import tpu as pltpu

D_MODEL = 1024
HEAD_DIM = 64
SB_HEADS = 8
SW_HEADS = 8
SW_KV_HEADS = 2
WINDOW = 128
D_FF = 4 * D_MODEL
PLE_DIM = 256
N_BUCKETS = 32
MAX_DISTANCE = 128
EPS = 1e-6

SB_W = SB_HEADS * HEAD_DIM
SW_QW = SW_HEADS * HEAD_DIM
SW_KVW = SW_KV_HEADS * HEAD_DIM
LANES = 128
HEADS_PER_TILE = LANES // HEAD_DIM

ROW_BLOCK = 512
COL_CHUNK = 512
SB_BLOCK = 128
SB_DEAD = -106.0
VMEM_LIMIT = 56 * 1024 * 1024

F32 = jnp.float32
BF16 = jnp.bfloat16


def _rms(x, g):
    r = lax.rsqrt(jnp.mean(x * x, axis=-1, keepdims=True) + EPS)
    return (x * r) * g


def _const_spec(shape):
    return pl.BlockSpec(shape, lambda *_: (0,) * len(shape))


def _params(*sem):
    return pltpu.CompilerParams(dimension_semantics=sem, vmem_limit_bytes=VMEM_LIMIT)


def _norm_proj_kernel(x_ref, g_ref, w_ref, qkva_ref, qb_ref, kvb_ref, gate_ref):
    h = _rms(x_ref[...], g_ref[...]).astype(BF16)

    def mm(lo, width):
        return jnp.dot(h, w_ref[:, lo:lo + width], preferred_element_type=F32)

    scale = HEAD_DIM ** -0.5
    qkva_ref[:, 0:SB_W] = (mm(0, SB_W) * scale).astype(BF16)
    qkva_ref[:, SB_W:2 * SB_W] = mm(SB_W, SB_W).astype(BF16)
    qkva_ref[:, 2 * SB_W:3 * SB_W] = mm(2 * SB_W, SB_W).astype(BF16)
    o3 = 3 * SB_W
    qb_ref[...] = (mm(o3, SW_QW) * scale).astype(BF16)
    o4 = o3 + SW_QW
    kv = mm(o4, 2 * SW_KVW)
    k, v = kv[:, :SW_KVW], kv[:, SW_KVW:]
    kvb_ref[:, 0 * LANES:1 * LANES] = k.astype(BF16)
    kvb_ref[:, 1 * LANES:2 * LANES] = pltpu.roll(k, HEAD_DIM, 1).astype(BF16)
    kvb_ref[:, 2 * LANES:3 * LANES] = v.astype(BF16)
    kvb_ref[:, 3 * LANES:4 * LANES] = pltpu.roll(v, HEAD_DIM, 1).astype(BF16)
    o6 = o4 + 2 * SW_KVW
    for c in range(2 * D_MODEL // COL_CHUNK):
        gate_ref[:, c * COL_CHUNK:(c + 1) * COL_CHUNK] = jax.nn.sigmoid(
            mm(o6 + c * COL_CHUNK, COL_CHUNK)).astype(BF16)


def _norm_proj(x, g, w):
    t = x.shape[0]
    n_in = w.shape[1]
    row = lambda width: pl.BlockSpec((ROW_BLOCK, width), lambda i: (i, 0))
    return pl.pallas_call(
        _norm_proj_kernel,
        grid=(t // ROW_BLOCK,),
        in_specs=[row(D_MODEL), _const_spec((1, D_MODEL)), _const_spec((D_MODEL, n_in))],
        out_specs=[row(3 * SB_W), row(SW_QW), row(4 * LANES), row(2 * D_MODEL)],
        out_shape=[jax.ShapeDtypeStruct((t, 3 * SB_W), BF16),
                   jax.ShapeDtypeStruct((t, SW_QW), BF16),
                   jax.ShapeDtypeStruct((t, 4 * LANES), BF16),
                   jax.ShapeDtypeStruct((t, 2 * D_MODEL), BF16)],
        compiler_params=_params("parallel"),
        name="norm_proj",
    )(x, g, w)


def _sb_kernel(q_ref, k_ref, v_ref, tu_ref, o_ref, acc_ref, carry_ref):
    i = pl.program_id(2)
    blk = SB_BLOCK
    q = q_ref[...]
    lane = lax.broadcasted_iota(jnp.int32, (1, LANES), 1)
    lo_lanes = lane < HEAD_DIM
    zero = jnp.zeros((), BF16)
    q_heads = (jnp.where(lo_lanes, q, zero), jnp.where(lo_lanes, zero, q))
    tu = tu_ref[...]
    row = lax.broadcasted_iota(jnp.int32, (blk, blk), 0)
    col = lax.broadcasted_iota(jnp.int32, (blk, blk), 1)
    causal = col < row

    def chunk(start, diag):
        kc = k_ref[pl.ds(start, blk), :]
        vc = v_ref[pl.ds(start, blk), :]
        v_heads = (jnp.where(lo_lanes, vc, zero), jnp.where(lo_lanes, zero, vc))
        out = None
        for h in range(HEADS_PER_TILE):
            z = lax.dot_general(q_heads[h], kc, (((1,), (1,)), ((), ())),
                                preferred_element_type=F32)
            sp = jnp.log(1.0 + jnp.exp(-jnp.abs(z)))
            log_beta = jnp.minimum(z, 0.0) - sp
            log_1mb = jnp.minimum(-z, 0.0) - sp
            if diag:
                log_1mb = jnp.where(causal, log_1mb, 0.0)
            hi = log_1mb.astype(BF16)
            lo = (log_1mb - hi.astype(F32)).astype(BF16)
            sums = (jnp.dot(hi, tu, preferred_element_type=F32)
                    + jnp.dot(lo, tu, preferred_element_type=F32))
            carry = carry_ref[h]
            a = jnp.exp(log_beta + sums[:, :blk] + carry)
            if diag:
                a = jnp.where(causal, a, 0.0)
            pv = jnp.dot(a.astype(BF16), v_heads[h], preferred_element_type=F32)
            out = pv if out is None else out + pv
            carry_ref[h] = carry + sums[:, blk:]
        acc_ref[...] += out

    def alive():
        return jnp.max(jnp.maximum(carry_ref[0], carry_ref[1])) > SB_DEAD

    acc_ref[...] = jnp.zeros_like(acc_ref)
    carry_ref[...] = jnp.zeros_like(carry_ref)
    chunk(pl.multiple_of(i * blk, blk), True)

    def cond(state):
        j, live = state
        return jnp.logical_and(j >= 0, live)

    def body(state):
        j, _ = state
        chunk(pl.multiple_of(j * blk, blk), False)
        return j - 1, alive()

    lax.while_loop(cond, body, (i - 1, alive()))
    o_ref[...] = acc_ref[...].astype(o_ref.dtype)


def _sb_attention(qkva, tu, batch, seq):
    t = qkva.shape[0]
    pairs = SB_HEADS // HEADS_PER_TILE
    nq = seq // SB_BLOCK
    q_spec = pl.BlockSpec((SB_BLOCK, LANES), lambda b, p, i: (b * nq + i, p))
    k_spec = pl.BlockSpec((seq, LANES), lambda b, p, i: (b, pairs + p))
    v_spec = pl.BlockSpec((seq, LANES), lambda b, p, i: (b, 2 * pairs + p))
    return pl.pallas_call(
        _sb_kernel,
        grid=(batch, pairs, nq),
        in_specs=[q_spec, k_spec, v_spec, _const_spec(tu.shape)],
        out_specs=pl.BlockSpec((SB_BLOCK, LANES), lambda b, p, i: (b * nq + i, p)),
        out_shape=jax.ShapeDtypeStruct((t, SB_W), BF16),
        scratch_shapes=[pltpu.VMEM((SB_BLOCK, LANES), F32),
                        pltpu.VMEM((HEADS_PER_TILE, SB_BLOCK, LANES), F32)],
        compiler_params=_params("parallel", "parallel", "arbitrary"),
        name="sb_attention",
    )(qkva, qkva, qkva, tu)


def _bias_kernel(bucket_ref, rel_ref, o_ref):
    bucket = bucket_ref[...]
    for h in range(SW_HEADS):
        acc = jnp.zeros(bucket.shape, F32)
        for b in range(N_BUCKETS):
            acc = jnp.where(bucket == b, rel_ref[b, h], acc)
        o_ref[h] = acc


def _window_bias(rel_bias):
    r = np.arange(WINDOW)[:, None]
    c = np.arange(WINDOW)[None, :]
    dist = np.where(c > r, WINDOW + r - c, r - c)
    max_exact = N_BUCKETS // 2
    df = np.maximum(dist, 1).astype(np.float32)
    large = max_exact + (np.log(df / max_exact) / math.log(MAX_DISTANCE / max_exact)
                         * (N_BUCKETS - max_exact)).astype(np.int32)
    large = np.minimum(large, N_BUCKETS - 1)
    bucket = jnp.asarray(np.where(dist < max_exact, dist, large).astype(np.int32))
    return pl.pallas_call(
        _bias_kernel,
        in_specs=[pl.BlockSpec(memory_space=pltpu.VMEM),
                  pl.BlockSpec(memory_space=pltpu.SMEM)],
        out_specs=pl.BlockSpec(memory_space=pltpu.VMEM),
        out_shape=jax.ShapeDtypeStruct((SW_HEADS, WINDOW, WINDOW), F32),
        name="window_bias",
    )(bucket, rel_bias)


def _swa_kernel(sink_ref, q_ref, kvp_ref, kvc_ref, bias_ref, o_ref):
    i = pl.program_id(1)
    blk = WINDOW
    lane = lax.broadcasted_iota(jnp.int32, (1, LANES), 1)
    lo_lanes = lane < HEAD_DIM
    zero = jnp.zeros((), BF16)
    row = lax.broadcasted_iota(jnp.int32, (blk, blk), 0)
    col = lax.broadcasted_iota(jnp.int32, (blk, blk), 1)
    from_prev = col > row
    valid = jnp.logical_or(jnp.logical_not(from_prev), i > 0)

    def variants(ref, base):
        plain = ref[:, base * LANES:(base + 1) * LANES]
        swapped = ref[:, (base + 1) * LANES:(base + 2) * LANES]
        return plain, swapped

    kp, kp_s = variants(kvp_ref, 0)
    kc, kc_s = variants(kvc_ref, 0)
    vp, vp_s = variants(kvp_ref, 2)
    vc, vc_s = variants(kvc_ref, 2)

    def dot_t(a, b):
        return lax.dot_general(a, b, (((1,), (1,)), ((), ())), preferred_element_type=F32)

    for pair in range(SW_HEADS // HEADS_PER_TILE):
        group = pair // (SW_HEADS // SW_KV_HEADS // HEADS_PER_TILE)
        q = q_ref[:, pair * LANES:(pair + 1) * LANES]
        out = None
        for half in range(HEADS_PER_TILE):
            head = pair * HEADS_PER_TILE + half
            use_plain = (group == half)
            k_prev, k_cur = (kp, kc) if use_plain else (kp_s, kc_s)
            v_prev, v_cur = (vp, vc) if use_plain else (vp_s, vc_s)
            mine = lo_lanes if half == 0 else jnp.logical_not(lo_lanes)
            qh = jnp.where(mine, q, zero)
            s = jnp.where(from_prev, dot_t(qh, k_prev), dot_t(qh, k_cur)) + bias_ref[head]
            s = jnp.where(valid, s, -jnp.inf)
            sink = sink_ref[head]
            m = jnp.maximum(jnp.max(s, axis=-1, keepdims=True), sink)
            e = jnp.exp(s - m)
            denom = jnp.sum(e, axis=-1, keepdims=True) + jnp.exp(sink - m)
            pr = e / denom
            p_prev = jnp.where(from_prev, pr, 0.0).astype(BF16)
            p_cur = jnp.where(from_prev, 0.0, pr).astype(BF16)
            pv = (jnp.dot(p_prev, jnp.where(mine, v_prev, zero), preferred_element_type=F32)
                  + jnp.dot(p_cur, jnp.where(mine, v_cur, zero), preferred_element_type=F32))
            out = pv if out is None else out + pv
        o_ref[:, pair * LANES:(pair + 1) * LANES] = out.astype(o_ref.dtype)


def _swa_attention(qb, kvb, sinks, bias, batch, seq):
    t = qb.shape[0]
    nq = seq // WINDOW
    cur = lambda b, i: (b * nq + i, 0)
    prev = lambda b, i: (b * nq + jnp.maximum(i - 1, 0), 0)
    return pl.pallas_call(
        _swa_kernel,
        grid=(batch, nq),
        in_specs=[pl.BlockSpec(memory_space=pltpu.SMEM),
                  pl.BlockSpec((WINDOW, SW_QW), cur),
                  pl.BlockSpec((WINDOW, 4 * LANES), prev),
                  pl.BlockSpec((WINDOW, 4 * LANES), cur),
                  _const_spec((SW_HEADS, WINDOW, WINDOW))],
        out_specs=pl.BlockSpec((WINDOW, SW_QW), cur),
        out_shape=jax.ShapeDtypeStruct((t, SW_QW), BF16),
        compiler_params=_params("parallel", "parallel"),
        name="swa_attention",
    )(sinks, qb, kvb, kvb, bias)


def _merge_kernel(x_ref, ya_ref, yb_ref, gate_ref, wa_ref, wb_ref, wo_ref, o_ref):
    y_a = jnp.dot(ya_ref[...], wa_ref[...], preferred_element_type=F32)
    y_b = jnp.dot(yb_ref[...], wb_ref[...], preferred_element_type=F32)
    merged = (gate_ref[:, :D_MODEL].astype(F32) * y_a
              + gate_ref[:, D_MODEL:].astype(F32) * y_b).astype(BF16)
    o_ref[...] = x_ref[...] + jnp.dot(merged, wo_ref[...], preferred_element_type=F32)


def _merge(x, ya, yb, gates, wa, wb, wo):
    t = x.shape[0]
    row = lambda width: pl.BlockSpec((ROW_BLOCK, width), lambda i: (i, 0))
    return pl.pallas_call(
        _merge_kernel,
        grid=(t // ROW_BLOCK,),
        in_specs=[row(D_MODEL), row(SB_W), row(SW_QW), row(2 * D_MODEL),
                  _const_spec(wa.shape), _const_spec(wb.shape), _const_spec(wo.shape)],
        out_specs=row(D_MODEL),
        out_shape=jax.ShapeDtypeStruct((t, D_MODEL), F32),
        compiler_params=_params("parallel"),
        name="merge",
    )(x, ya, yb, gates, wa, wb, wo)


def _mlp_kernel(x_ref, g_ref, w1_ref, w2_ref, o_ref):
    x = x_ref[...]
    h = _rms(x, g_ref[...]).astype(BF16)
    acc = x
    for c in range(D_FF // COL_CHUNK):
        sl = slice(c * COL_CHUNK, (c + 1) * COL_CHUNK)
        u = jnp.maximum(jnp.dot(h, w1_ref[:, sl], preferred_element_type=F32), 0.0)
        acc = acc + jnp.dot((u * u).astype(BF16), w2_ref[sl, :], preferred_element_type=F32)
    o_ref[...] = acc


def _mlp(x, g, w1, w2):
    t = x.shape[0]
    row = pl.BlockSpec((ROW_BLOCK, D_MODEL), lambda i: (i, 0))
    return pl.pallas_call(
        _mlp_kernel,
        grid=(t // ROW_BLOCK,),
        in_specs=[row, _const_spec((1, D_MODEL)), _const_spec(w1.shape), _const_spec(w2.shape)],
        out_specs=row,
        out_shape=jax.ShapeDtypeStruct((t, D_MODEL), F32),
        compiler_params=_params("parallel"),
        name="mlp",
    )(x, g, w1, w2)


def _ple_kernel(x_ref, p_ref, g_ref, wpe_ref, wpg_ref, gf_ref, o_ref, *, final_norm):
    x = x_ref[...]
    pe = jnp.dot(p_ref[...].astype(BF16), wpe_ref[...], preferred_element_type=F32)
    h = _rms(x, g_ref[...]).astype(BF16)
    gate = jax.nn.sigmoid(jnp.dot(h, wpg_ref[...], preferred_element_type=F32))
    y = x + pe * gate
    o_ref[...] = _rms(y, gf_ref[...]) if final_norm else y


def _ple(x, p, g, wpe, wpg, g_final, final_norm):
    t = x.shape[0]
    row = lambda width: pl.BlockSpec((ROW_BLOCK, width), lambda i: (i, 0))
    return pl.pallas_call(
        functools.partial(_ple_kernel, final_norm=final_norm),
        grid=(t // ROW_BLOCK,),
        in_specs=[row(D_MODEL), row(PLE_DIM), _const_spec((1, D_MODEL)),
                  _const_spec(wpe.shape), _const_spec(wpg.shape), _const_spec((1, D_MODEL))],
        out_specs=row(D_MODEL),
        out_shape=jax.ShapeDtypeStruct((t, D_MODEL), F32),
        compiler_params=_params("parallel"),
        name="ple",
    )(x, p, g, wpe, wpg, g_final)


def _suffix_matrix():
    j = np.arange(SB_BLOCK)[:, None]
    s = np.arange(SB_BLOCK)[None, :]
    strict = (j > s).astype(np.float32)
    return jnp.asarray(np.concatenate([strict, np.ones_like(strict)], axis=1), BF16)


def kernel(x, p, w_in, w_up_a, w_up_b, w_o, w_ff1, w_ff2, w_pe, w_pg,
           g_mix, g_mlp, g_pe, g_final, sinks, rel_bias):
    batch, seq, d = x.shape
    depth = w_in.shape[0]
    t = batch * seq
    xt = x.reshape(t, d)
    bias = _window_bias(rel_bias)
    tu = _suffix_matrix()
    gf = g_final.reshape(1, d)
    for i in range(depth):
        qkva, qb, kvb, gates = _norm_proj(xt, g_mix[i].reshape(1, d), w_in[i].astype(BF16))
        ya = _sb_attention(qkva, tu, batch, seq)
        yb = _swa_attention(qb, kvb, sinks[i], bias, batch, seq)
        xt = _merge(xt, ya, yb, gates, w_up_a[i].astype(BF16), w_up_b[i].astype(BF16),
                    w_o[i].astype(BF16))
        xt = _mlp(xt, g_mlp[i].reshape(1, d), w_ff1[i].astype(BF16), w_ff2[i].astype(BF16))
        xt = _ple(xt, p[i].reshape(t, PLE_DIM), g_pe[i].reshape(1, d), w_pe[i].astype(BF16),
                  w_pg[i].astype(BF16), gf, final_norm=(i == depth - 1))
    return xt.reshape(batch, seq, d)
```
